```python
import math
import jax, jax.numpy as jnp
from jax import lax
import numpy as np

D_MODEL = 1024
BATCH = 16
SEQ = 2048
DEPTH = 2

HEAD_DIM = 64
QBLOCK = 128
EPS = 1e-5
SWA_HEADS = 8
SWA_KV_HEADS = 2
SWA_WINDOW = 128
DIFF_HEADS = 4
NSA_HEADS = 8
NSA_KV_HEADS = 2
CMP_BLOCK = 32
CMP_STRIDE = 16
CMP_HIDDEN = 256
SEL_BLOCK = 64
SEL_TOPN = 16
SEL_QCHUNK = 32
NSA_WINDOW = 512
BRANCH_WIDTH = 512
N_BRANCHES = 3
N_EXPERTS = 32
TOP_K = 4
D_EXPERT = D_MODEL
SWIGLU_LIMIT = 7.0
SWIGLU_ALPHA = 1.702
MOE_BLOCK = 128
A_Q = SWA_HEADS * HEAD_DIM
A_KV = SWA_KV_HEADS * HEAD_DIM
B_QK = DIFF_HEADS * 2 * HEAD_DIM
B_V = DIFF_HEADS * 2 * HEAD_DIM
C_Q = NSA_HEADS * HEAD_DIM
C_KV = NSA_KV_HEADS * HEAD_DIM
NSA_GATES = NSA_HEADS * 3
MERGE_GATES = N_BRANCHES * D_MODEL
IN_SPLITS = (A_Q, A_KV, A_KV, B_QK, B_QK, B_V, C_Q, C_KV, C_KV, C_KV, C_KV, C_KV, C_KV, NSA_GATES, MERGE_GATES)
IN_WIDTH = sum(IN_SPLITS)

kernel_name = 'hybrid_swa_diff_nsa_moe_adaln'


def rms_norm(x, g):
    xf = x.astype(jnp.float32)
    y = xf * lax.rsqrt(jnp.mean(xf * xf, axis=-1, keepdims=True) + EPS)
    return (y * g.astype(jnp.float32)).astype(x.dtype)


def alibi_slopes(n):
    return 2.0 ** (-8.0 * jnp.arange(1, n + 1, dtype=jnp.float32) / n)


def banded_attention(q, k, v, window, slopes, sinks=None):
    B, S, Hkv, G, d = q.shape
    nb = S // QBLOCK
    n_prev = -(-(window - 1) // QBLOCK)
    ctx = (n_prev + 1) * QBLOCK
    pad = ((0, 0), (n_prev * QBLOCK, 0), (0, 0), (0, 0))
    kp = jnp.pad(k, pad)
    vp = jnp.pad(v, pad)
    qb = q.reshape(B, nb, QBLOCK, Hkv, G, d).swapaxes(0, 1)
    scale = d ** -0.5

    def block(args):
        qi, i = args
        start = i * QBLOCK
        kc = lax.dynamic_slice_in_dim(kp, start, ctx, axis=1)
        vc = lax.dynamic_slice_in_dim(vp, start, ctx, axis=1)
        s = jnp.einsum('bqhgd,bkhd->bhgqk', qi, kc).astype(jnp.float32) * scale
        t = start + jnp.arange(QBLOCK)
        pos = start - n_prev * QBLOCK + jnp.arange(ctx)
        dist = t[:, None] - pos[None, :]
        valid = (dist >= 0) & (dist < window) & (pos[None, :] >= 0)
        s = s - slopes[:, :, None, None] * dist.astype(jnp.float32)
        s = jnp.where(valid, s, -jnp.inf)
        if sinks is None:
            p = jax.nn.softmax(s, axis=-1)
        else:
            lse = jnp.logaddexp(jax.nn.logsumexp(s, axis=-1, keepdims=True), sinks[:, :, None, None])
            p = jnp.exp(s - lse)
        return jnp.einsum('bhgqk,bkhd->bqhgd', p.astype(vc.dtype), vc)

    o = lax.map(block, (qb, jnp.arange(nb)))
    return o.swapaxes(0, 1).reshape(B, S, Hkv, G, d)


def diff_attention(q, k, v, lam_params, subln_g, slopes, lam_init):
    B, S, H, _, d = q.shape
    nb = S // QBLOCK
    lp = lam_params.astype(jnp.float32)
    lam = jnp.exp(jnp.sum(lp[0] * lp[1])) - jnp.exp(jnp.sum(lp[2] * lp[3])) + lam_init
    qb = q.reshape(B, nb, QBLOCK, H, 2, d).swapaxes(0, 1)
    kpos = jnp.arange(S)
    scale = d ** -0.5

    def block(args):
        qi, i = args
        s = jnp.einsum('bqhcd,bkhcd->bhcqk', qi, k).astype(jnp.float32) * scale
        t = i * QBLOCK + jnp.arange(QBLOCK)
        dist = t[:, None] - kpos[None, :]
        s = s - slopes[:, None, None, None] * dist.astype(jnp.float32)
        s = jnp.where(dist >= 0, s, -jnp.inf)
        p = jax.nn.softmax(s, axis=-1)
        w = p[:, :, 0] - lam * p[:, :, 1]
        return jnp.einsum('bhqk,bkhe->bqhe', w.astype(v.dtype), v)

    o = lax.map(block, (qb, jnp.arange(nb)))
    o = o.swapaxes(0, 1).reshape(B, S, H, 2 * d)
    return rms_norm(o, subln_g) * (1.0 - lam_init)


def compress_blocks(x, pos, w1, b1, w2, b2):
    B, S, Hkv, d = x.shape
    nc = (S - CMP_BLOCK) // CMP_STRIDE + 1
    idx = jnp.arange(nc)[:, None] * CMP_STRIDE + jnp.arange(CMP_BLOCK)[None, :]
    blk = x[:, idx] + pos[:, None, :]
    blk = blk.transpose(0, 1, 3, 2, 4).reshape(B, nc, Hkv, CMP_BLOCK * d)
    hid = jax.nn.gelu(blk @ w1 + b1)
    return hid @ w2 + b2


def nsa_attention(q, k_cmp, v_cmp, k_slc, v_slc, k_win, v_win, gates, cmp_pos, cmp_w1, cmp_b1, cmp_w2, cmp_b2, slopes):
    B, S, Hkv, G, d = q.shape
    scale = d ** -0.5
    t = jnp.arange(S)
    kc = compress_blocks(k_cmp, cmp_pos[0], cmp_w1[0], cmp_b1[0], cmp_w2[0], cmp_b2[0])
    vc = compress_blocks(v_cmp, cmp_pos[1], cmp_w1[1], cmp_b1[1], cmp_w2[1], cmp_b2[1])
    nc = kc.shape[1]
    cstart = jnp.arange(nc) * CMP_STRIDE
    cvalid = (cstart[None, :] + CMP_BLOCK - 1) <= t[:, None]
    s = jnp.einsum('bqhgd,bchd->bhgqc', q, kc).astype(jnp.float32) * scale
    p_cmp = jax.nn.softmax(jnp.where(cvalid, s, -1e30), axis=-1) * cvalid
    o_cmp = jnp.einsum('bhgqc,bchd->bqhgd', p_cmp.astype(vc.dtype), vc)
    nsb = S // SEL_BLOCK
    sstart = jnp.arange(nsb) * SEL_BLOCK
    overlap = jnp.clip(jnp.minimum(cstart[:, None] + CMP_BLOCK, sstart[None, :] + SEL_BLOCK)
                       - jnp.maximum(cstart[:, None], sstart[None, :]), 0).astype(jnp.float32) / CMP_BLOCK
    imp = jnp.einsum('bhgqc,cn->bhqn', p_cmp, overlap)
    cur = t // SEL_BLOCK
    blk = jnp.arange(nsb)
    causal_blk = blk[None, :] <= cur[:, None]
    forced = (blk[None, :] == 0) | (blk[None, :] == cur[:, None]) | (blk[None, :] == cur[:, None] - 1)
    score = jnp.where(forced, jnp.inf, jnp.where(causal_blk, imp, -jnp.inf))
    n_sel = min(SEL_TOPN, nsb)
    top_s, top_i = lax.top_k(score, n_sel)
    sel_ok = top_s > -jnp.inf
    kb = k_slc.reshape(B, nsb, SEL_BLOCK, Hkv, d).transpose(0, 3, 1, 2, 4)
    vb = v_slc.reshape(B, nsb, SEL_BLOCK, Hkv, d).transpose(0, 3, 1, 2, 4)
    nq = S // SEL_QCHUNK
    qch = q.reshape(B, nq, SEL_QCHUNK, Hkv, G, d).swapaxes(0, 1)
    ich = top_i.reshape(B, Hkv, nq, SEL_QCHUNK, n_sel).transpose(2, 0, 1, 3, 4)
    och = sel_ok.reshape(B, Hkv, nq, SEL_QCHUNK, n_sel).transpose(2, 0, 1, 3, 4)
    bidx = jnp.arange(B)[:, None, None, None]
    hidx = jnp.arange(Hkv)[None, :, None, None]

    def chunk(args):
        qi, ii, oki, cidx = args
        kg = kb[bidx, hidx, ii]
        vg = vb[bidx, hidx, ii]
        sc = jnp.einsum('bqhgd,bhqnld->bhgqnl', qi, kg).astype(jnp.float32) * scale
        tq = cidx * SEL_QCHUNK + jnp.arange(SEL_QCHUNK)
        kpos = ii[..., None] * SEL_BLOCK + jnp.arange(SEL_BLOCK)
        dist = tq[None, None, :, None, None] - kpos
        ok = (dist >= 0) & oki[..., None]
        sc = sc - slopes[None, :, :, None, None, None] * dist[:, :, None].astype(jnp.float32)
        sc = jnp.where(ok[:, :, None], sc, -jnp.inf)
        shp = sc.shape
        pr = jax.nn.softmax(sc.reshape(shp[:4] + (shp[4] * shp[5],)), axis=-1).reshape(shp)
        return jnp.einsum('bhgqnl,bhqnld->bqhgd', pr.astype(vg.dtype), vg)

    o_slc = lax.map(chunk, (qch, ich, och, jnp.arange(nq)))
    o_slc = o_slc.swapaxes(0, 1).reshape(B, S, Hkv, G, d)
    o_win = banded_attention(q, k_win, v_win, NSA_WINDOW, slopes)
    g = jax.nn.sigmoid(gates)
    return g[..., 0:1] * o_cmp + g[..., 1:2] * o_slc + g[..., 2:3] * o_win


def hybrid_mixer(h, w_in, b_in, sinks, diff_lambda, diff_subln_g, cmp_pos, cmp_w1, cmp_b1, cmp_w2, cmp_b2, w_branch, w_out, lam_init):
    B, S, _ = h.shape
    z = h @ w_in + b_in
    offs = np.cumsum(IN_SPLITS)[:-1].tolist()
    (qa, ka, va, qb, kb, vb, qc, kcm, vcm, ksl, vsl, kwn, vwn, g_nsa, g_merge) = jnp.split(z, offs, axis=-1)
    ga = SWA_HEADS // SWA_KV_HEADS
    o_a = banded_attention(qa.reshape(B, S, SWA_KV_HEADS, ga, HEAD_DIM),
                           ka.reshape(B, S, SWA_KV_HEADS, HEAD_DIM), va.reshape(B, S, SWA_KV_HEADS, HEAD_DIM),
                           SWA_WINDOW, alibi_slopes(SWA_HEADS).reshape(SWA_KV_HEADS, ga),
                           sinks.astype(jnp.float32).reshape(SWA_KV_HEADS, ga)).reshape(B, S, BRANCH_WIDTH)
    o_b = diff_attention(qb.reshape(B, S, DIFF_HEADS, 2, HEAD_DIM), kb.reshape(B, S, DIFF_HEADS, 2, HEAD_DIM),
                         vb.reshape(B, S, DIFF_HEADS, 2 * HEAD_DIM), diff_lambda, diff_subln_g,
                         alibi_slopes(DIFF_HEADS), lam_init).reshape(B, S, BRANCH_WIDTH)
    gc = NSA_HEADS // NSA_KV_HEADS
    kv = lambda a: a.reshape(B, S, NSA_KV_HEADS, HEAD_DIM)
    o_c = nsa_attention(qc.reshape(B, S, NSA_KV_HEADS, gc, HEAD_DIM), kv(kcm), kv(vcm), kv(ksl), kv(vsl), kv(kwn), kv(vwn),
                        g_nsa.reshape(B, S, NSA_KV_HEADS, gc, 3), cmp_pos, cmp_w1, cmp_b1, cmp_w2, cmp_b2,
                        alibi_slopes(NSA_HEADS).reshape(NSA_KV_HEADS, gc)).reshape(B, S, BRANCH_WIDTH)
    gm = jax.nn.sigmoid(g_merge.reshape(B, S, N_BRANCHES, D_MODEL))
    merged = gm[:, :, 0] * (o_a @ w_branch[0]) + gm[:, :, 1] * (o_b @ w_branch[1]) + gm[:, :, 2] * (o_c @ w_branch[2])
    return merged @ w_out


def moe_ffn(h, router_w, router_b, w1, b1, w2, b2):
    B, S, D = h.shape
    N = B * S
    xt = h.reshape(N, D)
    logits = (xt @ router_w + router_b).astype(jnp.float32)
    top_v, top_e = lax.top_k(logits, TOP_K)
    gate = jax.nn.softmax(top_v, axis=-1)
    A = N * TOP_K
    e_flat = top_e.reshape(A)
    order = jnp.argsort(e_flat)
    e_sorted = e_flat[order]
    tok_sorted = order // TOP_K
    g_sorted = gate.reshape(A)[order]
    counts = jnp.bincount(e_flat, length=N_EXPERTS)
    group_start = jnp.cumsum(counts) - counts
    padded = (counts + MOE_BLOCK - 1) // MOE_BLOCK * MOE_BLOCK
    pad_end = jnp.cumsum(padded)
    pad_start = pad_end - padded
    dest = pad_start[e_sorted] + (jnp.arange(A) - group_start[e_sorted])
    n_blocks = -(-A // MOE_BLOCK) + N_EXPERTS
    P = n_blocks * MOE_BLOCK
    row_tok = jnp.zeros((P,), jnp.int32).at[dest].set(tok_sorted.astype(jnp.int32))
    row_w = jnp.zeros((P,), jnp.float32).at[dest].set(g_sorted)
    block_expert = jnp.clip(jnp.searchsorted(pad_end, jnp.arange(n_blocks) * MOE_BLOCK, side='right'), 0, N_EXPERTS - 1)

    def expert_block(args):
        rows, e = args
        xb = xt[rows]
        hb = xb @ w1[e] + b1[e]
        g, u = hb[:, :D_EXPERT], hb[:, D_EXPERT:]
        g = jnp.minimum(g, SWIGLU_LIMIT)
        u = jnp.clip(u, -SWIGLU_LIMIT, SWIGLU_LIMIT)
        glu = g * jax.nn.sigmoid(g * SWIGLU_ALPHA)
        return ((u + 1.0) * glu) @ w2[e] + b2[e]

    y = lax.map(expert_block, (row_tok.reshape(n_blocks, MOE_BLOCK), block_expert)).reshape(P, D)
    out = jnp.zeros((N, D), y.dtype).at[row_tok].add(y * row_w[:, None].astype(y.dtype))
    return out.reshape(B, S, D)


def setup_inputs(seed: int = 0) -> dict:
    key = jax.random.key(seed)
    ks = jax.random.split(key, 32)
    L, D, E, F = DEPTH, D_MODEL, N_EXPERTS, D_EXPERT
    nrm = lambda k, shape, s: jax.random.normal(k, shape, jnp.float32) * s
    return {
        'x': nrm(ks[0], (BATCH, SEQ, D), 1.0),
        'c': nrm(ks[1], (BATCH, D), 1.0),
        'mod_w': nrm(ks[2], (L, D, 6 * D), 0.5 * D ** -0.5),
        'mod_b': nrm(ks[3], (L, 6 * D), 0.02),
        'norm1_g': 1.0 + nrm(ks[4], (L, D), 0.1),
        'norm2_g': 1.0 + nrm(ks[5], (L, D), 0.1),
        'w_in': nrm(ks[6], (L, D, IN_WIDTH), D ** -0.5),
        'b_in': nrm(ks[7], (L, IN_WIDTH), 0.02),
        'sinks': nrm(ks[8], (L, SWA_HEADS), 0.5),
        'diff_lambda': nrm(ks[9], (L, 4, HEAD_DIM), 0.1),
        'diff_subln_g': 1.0 + nrm(ks[10], (L, 2 * HEAD_DIM), 0.1),
        'cmp_pos': nrm(ks[11], (L, 2, CMP_BLOCK, HEAD_DIM), 0.1),
        'cmp_w1': nrm(ks[12], (L, 2, CMP_BLOCK * HEAD_DIM, CMP_HIDDEN), (CMP_BLOCK * HEAD_DIM) ** -0.5),
        'cmp_b1': nrm(ks[13], (L, 2, CMP_HIDDEN), 0.02),
        'cmp_w2': nrm(ks[14], (L, 2, CMP_HIDDEN, HEAD_DIM), CMP_HIDDEN ** -0.5),
        'cmp_b2': nrm(ks[15], (L, 2, HEAD_DIM), 0.02),
        'w_branch': nrm(ks[16], (L, N_BRANCHES, BRANCH_WIDTH, D), BRANCH_WIDTH ** -0.5),
        'w_out': nrm(ks[17], (L, D, D), D ** -0.5),
        'router_w': nrm(ks[18], (L, D, E), D ** -0.5),
        'router_b': nrm(ks[19], (L, E), 0.01),
        'exp_w1': nrm(ks[20], (L, E, D, 2 * F), D ** -0.5),
        'exp_b1': nrm(ks[21], (L, E, 2 * F), 0.02),
        'exp_w2': nrm(ks[22], (L, E, F, D), F ** -0.5),
        'exp_b2': nrm(ks[23], (L, E, D), 0.02),
        'final_g': 1.0 + nrm(ks[24], (D,), 0.1),
    }


def reference(x, c, mod_w, mod_b, norm1_g, norm2_g, w_in, b_in, sinks, diff_lambda, diff_subln_g, cmp_pos, cmp_w1, cmp_b1, cmp_w2, cmp_b2, w_branch, w_out, router_w, router_b, exp_w1, exp_b1, exp_w2, exp_b2, final_g):
    cs = jax.nn.silu(c)
    for l in range(DEPTH):
        mod = cs @ mod_w[l] + mod_b[l]
        sh1, sc1, g1, sh2, sc2, g2 = [m[:, None, :] for m in jnp.split(mod, 6, axis=-1)]
        lam_init = 0.8 - 0.6 * math.exp(-0.3 * l)
        h = rms_norm(x, norm1_g[l]) * (1.0 + sc1) + sh1
        x = x + g1 * hybrid_mixer(h, w_in[l], b_in[l], sinks[l], diff_lambda[l], diff_subln_g[l], cmp_pos[l], cmp_w1[l],
                                  cmp_b1[l], cmp_w2[l], cmp_b2[l], w_branch[l], w_out[l], lam_init)
        h = rms_norm(x, norm2_g[l]) * (1.0 + sc2) + sh2
        x = x + g2 * moe_ffn(h, router_w[l], router_b[l], exp_w1[l], exp_b1[l], exp_w2[l], exp_b2[l])
    return rms_norm(x, final_g)
```

```python
import functools
import math

import numpy as np
import jax
import jax.numpy as jnp
from jax import lax
from jax.experimental import pallas as pl
from jax.experimental.pallas import tpu as pltpu

F32 = jnp.float32
BF16 = jnp.bfloat16

HEAD_DIM = 64
EPS = 1e-5
SWA_HEADS = 8
SWA_WINDOW = 128
DIFF_HEADS = 4
NSA_HEADS = 8
CMP_BLOCK = 32
CMP_STRIDE = 16
CMP_HIDDEN = 256
SEL_BLOCK = 64
SEL_TOPN = 16
NSA_WINDOW = 512
N_EXPERTS = 32
TOP_K = 4
SWIGLU_LIMIT = 7.0
SWIGLU_ALPHA = 1.702

LANES = 128
QB = 128
VMEM_LIMIT = 56 * 1024 * 1024

NEG = -1e30

ZQ_WIDTH = 3584
CB_QA, CB_QB, CB_QC, CB_KB, CB_VB = 0, 4, 8, 12, 16
CB_KA, CB_VA, CB_KCM, CB_VCM, CB_KSL, CB_VSL, CB_KWN, CB_VWN = 20, 21, 22, 23, 24, 25, 26, 27
GN_WIDTH = 128
GM_WIDTH = 3072
HEAD_PERM = (0, 4, 1, 5, 2, 6, 3, 7)


def _dot(a, b):
    return jnp.dot(a, b, preferred_element_type=F32)


def _dot_nt(a, b):
    return lax.dot_general(a, b, (((1,), (1,)), ((), ())), preferred_element_type=F32)


def _params(sem, vmem=VMEM_LIMIT):
    return pltpu.CompilerParams(dimension_semantics=sem, vmem_limit_bytes=vmem)


def _alibi(n):
    return tuple(2.0 ** (-8.0 * (i + 1) / n) for i in range(n))


def _mod_kernel(c_ref, w_ref, b_ref, o_ref):
    c = c_ref[...]
    cs = c * jax.nn.sigmoid(c)
    o_ref[0] = jnp.dot(cs, w_ref[0], precision=lax.Precision.HIGHEST,
                       preferred_element_type=F32) + b_ref[0]


def _modulation(c, mod_w, mod_b):
    L, D, W = mod_w.shape
    B = c.shape[0]
    tn = 1536
    return pl.pallas_call(
        _mod_kernel,
        grid=(L, W // tn),
        in_specs=[pl.BlockSpec((B, D), lambda l, j: (0, 0)),
                  pl.BlockSpec((1, D, tn), lambda l, j: (l, 0, j)),
                  pl.BlockSpec((1, 1, tn), lambda l, j: (l, 0, j))],
        out_specs=pl.BlockSpec((1, B, tn), lambda l, j: (l, 0, j)),
        out_shape=jax.ShapeDtypeStruct((L, B, W), F32),
        compiler_params=_params(("parallel", "parallel")),
        name="modulation",
    )(c, mod_w, mod_b.reshape(L, 1, W))


def _inproj_kernel(x_ref, g_ref, sc_ref, sh_ref, w_ref, b_ref, zq_ref, gn_ref, gm_ref):
    xf = x_ref[...]
    var = jnp.mean(xf * xf, axis=-1, keepdims=True)
    h = xf * lax.rsqrt(var + EPS) * g_ref[...]
    h = h * (1.0 + sc_ref[0]) + sh_ref[0]
    hb = h.astype(BF16)
    cw = 512
    for c0 in range(0, ZQ_WIDTH, cw):
        zq_ref[:, c0:c0 + cw] = (_dot(hb, w_ref[:, c0:c0 + cw]) + b_ref[:, c0:c0 + cw]).astype(BF16)
    c0 = ZQ_WIDTH
    gn_ref[...] = _dot(hb, w_ref[:, c0:c0 + GN_WIDTH]) + b_ref[:, c0:c0 + GN_WIDTH]
    for j in range(GM_WIDTH // cw):
        c0 = ZQ_WIDTH + GN_WIDTH + j * cw
        gm_ref[:, j * cw:(j + 1) * cw] = _dot(hb, w_ref[:, c0:c0 + cw]) + b_ref[:, c0:c0 + cw]


def _inproj(x2, g, sc, sh, w, b, seq):
    N, D = x2.shape
    tm = 512
    per = seq // tm
    W = w.shape[1]
    return pl.pallas_call(
        _inproj_kernel,
        grid=(N // tm,),
        in_specs=[pl.BlockSpec((tm, D), lambda i: (i, 0)),
                  pl.BlockSpec((1, D), lambda i: (0, 0)),
                  pl.BlockSpec((1, 1, D), lambda i: (i // per, 0, 0)),
                  pl.BlockSpec((1, 1, D), lambda i: (i // per, 0, 0)),
                  pl.BlockSpec((D, W), lambda i: (0, 0), pipeline_mode=pl.Buffered(1)),
                  pl.BlockSpec((1, W), lambda i: (0, 0))],
        out_specs=[pl.BlockSpec((tm, ZQ_WIDTH), lambda i: (i, 0)),
                   pl.BlockSpec((tm, GN_WIDTH), lambda i: (i, 0)),
                   pl.BlockSpec((tm, GM_WIDTH), lambda i: (i, 0))],
        out_shape=[jax.ShapeDtypeStruct((N, ZQ_WIDTH), BF16),
                   jax.ShapeDtypeStruct((N, GN_WIDTH), F32),
                   jax.ShapeDtypeStruct((N, GM_WIDTH), F32)],
        compiler_params=_params(("parallel",)),
        name="inproj",
    )(x2, g, sc, sh, w, b)


def _half_masks(shape):
    lane = lax.broadcasted_iota(jnp.int32, shape, 1)
    return (lane < HEAD_DIM, lane >= HEAD_DIM)


def _online(s, v, m, l, acc):
    m_new = jnp.maximum(m, jnp.max(s, axis=-1, keepdims=True))
    alpha = jnp.exp(m - m_new)
    p = jnp.exp(s - m_new)
    l = alpha * l + jnp.sum(p, axis=-1, keepdims=True)
    acc = alpha * acc + _dot(p.astype(BF16), v)
    return m_new, l, acc


def _init_state(rows):
    return (jnp.full((rows, 1), NEG, F32), jnp.zeros((rows, 1), F32), jnp.zeros((rows, LANES), F32))


def _swa_kernel(sink_ref, q_ref, k_ref, v_ref, o_ref, *, slopes):
    i = pl.program_id(1)
    r = lax.broadcasted_iota(jnp.int32, (QB, QB), 0)
    c = lax.broadcasted_iota(jnp.int32, (QB, QB), 1)
    halves = _half_masks((QB, LANES))
    cur = pl.multiple_of(i * QB, QB)
    prev = pl.multiple_of(jnp.maximum(i - 1, 0) * QB, QB)
    k_cur, v_cur = k_ref[0, pl.ds(cur, QB), :], v_ref[0, pl.ds(cur, QB), :]
    k_prev, v_prev = k_ref[0, pl.ds(prev, QB), :], v_ref[0, pl.ds(prev, QB), :]
    zero = jnp.zeros_like(k_cur)
    d_cur = (r - c).astype(F32)
    d_prev = d_cur + float(QB)
    ok_cur = c <= r
    ok_prev = (c > r) & (i > 0)
    for p in range(4):
        q = q_ref[0, :, p * LANES:(p + 1) * LANES]
        out = jnp.zeros((QB, LANES), F32)
        for half in range(2):
            head = p + 4 * half
            sink = sink_ref[head]
            hm = halves[half]
            s1 = _dot_nt(q, jnp.where(hm, k_cur, zero))
            s0 = _dot_nt(q, jnp.where(hm, k_prev, zero))
            s1 = jnp.where(ok_cur, s1 - slopes[head] * d_cur, NEG)
            s0 = jnp.where(ok_prev, s0 - slopes[head] * d_prev, NEG)
            m = jnp.maximum(jnp.max(s1, axis=-1, keepdims=True), jnp.max(s0, axis=-1, keepdims=True))
            m = jnp.maximum(m, sink)
            p1 = jnp.exp(s1 - m)
            p0 = jnp.exp(s0 - m)
            l = jnp.sum(p1, axis=-1, keepdims=True) + jnp.sum(p0, axis=-1, keepdims=True) + jnp.exp(sink - m)
            o = _dot(p1.astype(BF16), jnp.where(hm, v_cur, zero)) + _dot(p0.astype(BF16), jnp.where(hm, v_prev, zero))
            out = out + o / l
        o_ref[0, :, p * LANES:(p + 1) * LANES] = out.astype(BF16)


def _swa(zq, sinks):
    B, S, _ = zq.shape
    kern = functools.partial(_swa_kernel, slopes=_alibi(SWA_HEADS))
    return pl.pallas_call(
        kern,
        grid=(B, S // QB),
        in_specs=[pl.BlockSpec(memory_space=pltpu.SMEM),
                  pl.BlockSpec((1, QB, 4 * LANES), lambda b, i: (b, i, CB_QA // 4)),
                  pl.BlockSpec((1, S, LANES), lambda b, i: (b, 0, CB_KA)),
                  pl.BlockSpec((1, S, LANES), lambda b, i: (b, 0, CB_VA))],
        out_specs=pl.BlockSpec((1, QB, 4 * LANES), lambda b, i: (b, i, 0)),
        out_shape=jax.ShapeDtypeStruct((B, S, 4 * LANES), BF16),
        compiler_params=_params(("parallel", "parallel")),
        name="swa",
    )(sinks, zq, zq, zq)


def _diff_kernel(slope_ref, lam_ref, g_ref, q_ref, k_ref, v_ref, o_ref, *, lam_init):
    h = pl.program_id(1)
    i = pl.program_id(2)
    slope = slope_ref[h]
    q = q_ref[0]
    r = lax.broadcasted_iota(jnp.int32, (QB, QB), 0)
    c = lax.broadcasted_iota(jnp.int32, (QB, QB), 1)
    halves = _half_masks((QB, LANES))
    d0 = (r - c).astype(F32)
    causal = c <= r

    def chunk(j, carry, masked):
        off = pl.multiple_of(j * QB, QB)
        kj = k_ref[0, pl.ds(off, QB), :]
        vj = v_ref[0, pl.ds(off, QB), :]
        dist = d0 + ((i - j) * QB).astype(F32)
        new = []
        for half in range(2):
            km = jnp.where(halves[half], kj, jnp.zeros_like(kj))
            s = _dot_nt(q, km) - slope * dist
            if masked:
                s = jnp.where(causal, s, NEG)
            new.append(_online(s, vj, *carry[half]))
        return tuple(new)

    carry = (_init_state(QB), _init_state(QB))
    carry = lax.fori_loop(0, i, lambda j, cr: chunk(j, cr, False), carry)
    (_, l0, a0), (_, l1, a1) = chunk(i, carry, True)
    lp = lam_ref[...]
    lam = (jnp.exp(jnp.sum(lp[0:1] * lp[1:2], keepdims=True))
           - jnp.exp(jnp.sum(lp[2:3] * lp[3:4], keepdims=True)) + lam_init)
    o = a0 / l0 - lam * (a1 / l1)
    var = jnp.mean(o * o, axis=-1, keepdims=True)
    o = o * lax.rsqrt(var + EPS) * g_ref[...] * (1.0 - lam_init)
    o_ref[0] = o.astype(BF16)


def _diff(zq, diff_lambda, subln_g, lam_init):
    B, S, _ = zq.shape
    H = DIFF_HEADS
    kern = functools.partial(_diff_kernel, lam_init=lam_init)
    slopes = jnp.asarray(_alibi(H), F32)
    return pl.pallas_call(
        kern,
        grid=(B, H, S // QB),
        in_specs=[pl.BlockSpec(memory_space=pltpu.SMEM),
                  pl.BlockSpec((4, HEAD_DIM), lambda b, h, i: (0, 0)),
                  pl.BlockSpec((1, 2 * HEAD_DIM), lambda b, h, i: (0, 0)),
                  pl.BlockSpec((1, QB, LANES), lambda b, h, i: (b, i, CB_QB + h)),
                  pl.BlockSpec((1, S, LANES), lambda b, h, i: (b, 0, CB_KB + h)),
                  pl.BlockSpec((1, S, LANES), lambda b, h, i: (b, 0, CB_VB + h))],
        out_specs=pl.BlockSpec((1, QB, LANES), lambda b, h, i: (b, i, h)),
        out_shape=jax.ShapeDtypeStruct((B, S, H * LANES), BF16),
        compiler_params=_params(("parallel", "parallel", "parallel")),
        name="diff_attn",
    )(slopes, diff_lambda, subln_g.reshape(1, -1), zq, zq, zq)


def _compress_kernel(x_ref, pos_ref, w1_ref, b1_ref, w2_ref, b2_ref, o_ref):
    x = x_ref[0, 0, 0].astype(F32)
    half = x.shape[1]
    nxt = pltpu.roll(x, x.shape[0] - 1, 0)
    left = (x + pos_ref[0, :, :half]).astype(BF16)
    right = (nxt + pos_ref[0, :, half:]).astype(BF16)
    hid = _dot(left, w1_ref[0, :half, :]) + _dot(right, w1_ref[0, half:, :]) + b1_ref[0]
    hid = jax.nn.gelu(hid)
    o_ref[0, 0, 0] = (_dot(hid.astype(BF16), w2_ref[0, 0]) + b2_ref[0, 0]).astype(BF16)


def _compress(xg, pos, w1, b1, w2p, b2p):
    _, B, _, NC, W = xg.shape
    return pl.pallas_call(
        _compress_kernel,
        grid=(2, B, 2),
        in_specs=[pl.BlockSpec((1, 1, 1, NC, W), lambda t, b, h: (t, b, h, 0, 0)),
                  pl.BlockSpec((1, 1, 2 * W), lambda t, b, h: (t, 0, 0)),
                  pl.BlockSpec((1, 2 * W, CMP_HIDDEN), lambda t, b, h: (t, 0, 0)),
                  pl.BlockSpec((1, 1, CMP_HIDDEN), lambda t, b, h: (t, 0, 0)),
                  pl.BlockSpec((1, 1, CMP_HIDDEN, LANES), lambda t, b, h: (t, h, 0, 0)),
                  pl.BlockSpec((1, 1, 1, LANES), lambda t, b, h: (t, h, 0, 0))],
        out_specs=pl.BlockSpec((1, 1, 1, NC, LANES), lambda t, b, h: (t, b, h, 0, 0)),
        out_shape=jax.ShapeDtypeStruct((2, B, 2, NC, LANES), BF16),
        compiler_params=_params(("parallel", "parallel", "parallel")),
        name="nsa_compress",
    )(xg, pos, w1, b1, w2p, b2p)


def _nsa_kernel(q_ref, ksl_ref, vsl_ref, kwn_ref, vwn_ref, kc_ref, vc_ref, gn_ref, ov_ref, o_ref,
                *, slopes, n_sel, n_blk, n_cmp):
    i = pl.program_id(1)
    G = 4
    M = G * QB
    NC = kc_ref.shape[3]
    halves = _half_masks((QB, LANES))
    qs = jnp.concatenate([q_ref[0, :, g * LANES:(g + 1) * LANES] for g in range(G)], axis=0)

    r = lax.broadcasted_iota(jnp.int32, (QB, LANES), 0)
    c = lax.broadcasted_iota(jnp.int32, (QB, LANES), 1)
    r4 = lax.broadcasted_iota(jnp.int32, (M, QB), 0) & (QB - 1)
    c4 = lax.broadcasted_iota(jnp.int32, (M, QB), 1)
    g4 = lax.broadcasted_iota(jnp.int32, (M, 1), 0) // QB
    d4 = (r4 - c4).astype(F32)
    slope_cols = []
    for half in range(2):
        col = jnp.zeros((M, 1), F32)
        for g in range(G):
            col = jnp.where(g4 == g, slopes[half * G + g], col)
        slope_cols.append(col)

    rc = lax.broadcasted_iota(jnp.int32, (QB, NC), 0)
    cc = lax.broadcasted_iota(jnp.int32, (QB, NC), 1)
    t_c = i * QB + rc
    cvalid = (cc * CMP_STRIDE + (CMP_BLOCK - 1) <= t_c) & (cc < n_cmp)
    cur = 2 * i + (r >= SEL_BLOCK).astype(jnp.int32)
    forced = (c == 0) | (c == cur) | (c == cur - 1)
    o_cmp, sel_bias = [], []
    for half in range(2):
        kcm = kc_ref[0, 0, half]
        vcm = vc_ref[0, 0, half]
        s = _dot_nt(qs, kcm)
        psum = jnp.zeros((QB, NC), F32)
        ps = []
        for g in range(G):
            sg = jnp.where(cvalid, s[g * QB:(g + 1) * QB], NEG)
            e = jnp.exp(sg - jnp.max(sg, axis=-1, keepdims=True))
            pg = jnp.where(cvalid, e / jnp.sum(e, axis=-1, keepdims=True), 0.0)
            psum = psum + pg
            ps.append(pg.astype(BF16))
        o_cmp.append(_dot(jnp.concatenate(ps, axis=0), vcm))
        p_hi = psum.astype(BF16)
        p_lo = (psum - p_hi.astype(F32)).astype(BF16)
        imp = _dot(p_hi, ov_ref[...]) + _dot(p_lo, ov_ref[...])
        score = jnp.where(forced, jnp.inf, jnp.where(c <= cur, imp, -jnp.inf))
        score = jnp.where(c < n_blk, score, -jnp.inf)
        cnt = jnp.zeros((QB, LANES), jnp.int32)
        for m in range(n_blk):
            col = jnp.broadcast_to(score[:, m:m + 1], (QB, LANES))
            beats = (col > score) | ((col == score) & (c > m))
            cnt = cnt + beats.astype(jnp.int32)
        sel = (cnt < n_sel) & (score > -jnp.inf)
        sel_bias.append(jnp.where(sel, 0.0, NEG).astype(BF16))

    def step(kj, vj, carry, biases, mask, base):
        new = []
        for half in range(2):
            km = jnp.where(halves[half], kj, jnp.zeros_like(kj))
            vm = jnp.where(halves[half], vj, jnp.zeros_like(vj))
            s = _dot_nt(qs, km) - slope_cols[half] * (d4 + base)
            if biases is not None:
                s = s + biases[half]
            if mask is not None:
                s = jnp.where(mask, s, NEG)
            new.append(_online(s, vm, *carry[half]))
        return tuple(new)

    def load(k_ref, v_ref, j):
        off = pl.multiple_of(j * QB, QB)
        return k_ref[0, pl.ds(off, QB), :], v_ref[0, pl.ds(off, QB), :]

    def slc_chunk(j, carry, masked):
        kj, vj = load(ksl_ref, vsl_ref, j)
        blk_of_key = 2 * j + (c >= SEL_BLOCK).astype(jnp.int32)
        expand = jnp.where(r == blk_of_key, 1.0, 0.0).astype(BF16)
        biases = []
        for half in range(2):
            b1 = _dot(sel_bias[half], expand)
            biases.append(jnp.concatenate([b1] * G, axis=0))
        mask = (c4 <= r4) if masked else None
        return step(kj, vj, carry, biases, mask, ((i - j) * QB).astype(F32))

    carry = (_init_state(M), _init_state(M))
    carry = lax.fori_loop(0, i, lambda j, cr: slc_chunk(j, cr, False), carry)
    slc = slc_chunk(i, carry, True)

    n_back = NSA_WINDOW // QB
    jo = jnp.maximum(i - n_back, 0)
    kj, vj = load(kwn_ref, vwn_ref, jo)
    carry = step(kj, vj, (_init_state(M), _init_state(M)), None, (c4 > r4) & (i >= n_back),
                 float(NSA_WINDOW))

    def win_chunk(j, cr):
        kj, vj = load(kwn_ref, vwn_ref, j)
        return step(kj, vj, cr, None, None, ((i - j) * QB).astype(F32))

    carry = lax.fori_loop(jnp.maximum(i - n_back + 1, 0), i, win_chunk, carry)
    kj, vj = load(kwn_ref, vwn_ref, i)
    win = step(kj, vj, carry, None, c4 <= r4, 0.0)

    gate = jax.nn.sigmoid(gn_ref[0])
    for g in range(G):
        out = jnp.zeros((QB, LANES), F32)
        rows = slice(g * QB, (g + 1) * QB)
        for half in range(2):
            col = (half * G + g) * 3
            _, l_s, a_s = slc[half]
            _, l_w, a_w = win[half]
            out = out + gate[:, col:col + 1] * o_cmp[half][rows]
            out = out + gate[:, col + 1:col + 2] * (a_s[rows] / l_s[rows])
            out = out + gate[:, col + 2:col + 3] * (a_w[rows] / l_w[rows])
        o_ref[0, :, g * LANES:(g + 1) * LANES] = out.astype(BF16)


def _nsa(zq, cmp_kv, gn, overlap):
    B, S, _ = zq.shape
    NC = cmp_kv.shape[3]
    n_blk = S // SEL_BLOCK
    kern = functools.partial(_nsa_kernel, slopes=_alibi(NSA_HEADS), n_sel=min(SEL_TOPN, n_blk),
                             n_blk=n_blk, n_cmp=(S - CMP_BLOCK) // CMP_STRIDE + 1)
    seq = lambda cb: pl.BlockSpec((1, S, LANES), lambda b, i: (b, 0, cb))
    return pl.pallas_call(
        kern,
        grid=(B, S // QB),
        in_specs=[pl.BlockSpec((1, QB, 4 * LANES), lambda b, i: (b, i, CB_QC // 4)),
                  seq(CB_KSL), seq(CB_VSL), seq(CB_KWN), seq(CB_VWN),
                  pl.BlockSpec((1, 1, 2, NC, LANES), lambda b, i: (0, b, 0, 0, 0)),
                  pl.BlockSpec((1, 1, 2, NC, LANES), lambda b, i: (1, b, 0, 0, 0)),
                  pl.BlockSpec((1, QB, GN_WIDTH), lambda b, i: (b, i, 0)),
                  pl.BlockSpec((NC, LANES), lambda b, i: (0, 0))],
        out_specs=pl.BlockSpec((1, QB, 4 * LANES), lambda b, i: (b, i, 0)),
        out_shape=jax.ShapeDtypeStruct((B, S, 4 * LANES), BF16),
        compiler_params=_params(("parallel", "parallel")),
        name="nsa_attn",
    )(zq, zq, zq, zq, zq, cmp_kv, cmp_kv, gn, overlap)


def _merge_kernel(oa_ref, ob_ref, oc_ref, gm_ref, x_ref, g1_ref, n2_ref, sc_ref, sh_ref,
                  wb_ref, wo_ref, rwh_ref, rwl_ref, rb_ref, xo_ref, h_ref, te_ref, tg_ref):
    D = x_ref.shape[1]
    merged = None
    for j, o_ref in enumerate((oa_ref, ob_ref, oc_ref)):
        pj = jax.nn.sigmoid(gm_ref[:, j * D:(j + 1) * D]) * _dot(o_ref[...], wb_ref[j])
        merged = pj if merged is None else merged + pj
    y = _dot(merged.astype(BF16), wo_ref[...])
    xn = x_ref[...] + g1_ref[0] * y
    xo_ref[...] = xn
    var = jnp.mean(xn * xn, axis=-1, keepdims=True)
    h = xn * lax.rsqrt(var + EPS) * n2_ref[...]
    h = h * (1.0 + sc_ref[0]) + sh_ref[0]
    h_ref[...] = h
    h_hi = h.astype(BF16)
    h_lo = (h - h_hi.astype(F32)).astype(BF16)
    logits = _dot(h_hi, rwh_ref[...]) + _dot(h_hi, rwl_ref[...]) + _dot(h_lo, rwh_ref[...]) + rb_ref[...]
    lane = lax.broadcasted_iota(jnp.int32, logits.shape, 1)
    lane_f = lane.astype(F32)
    vals, idxs = [], []
    for _ in range(TOP_K):
        m = jnp.max(logits, axis=-1, keepdims=True)
        idx = jnp.min(jnp.where(logits == m, lane_f, float(LANES)), axis=-1, keepdims=True)
        vals.append(m)
        idxs.append(idx)
        logits = jnp.where(lane_f == idx, -jnp.inf, logits)
    es = [jnp.exp(v - vals[0]) for v in vals]
    tot = es[0] + es[1] + es[2] + es[3]
    te = jnp.zeros(lane.shape, F32)
    tg = jnp.zeros(lane.shape, F32)
    for k in range(TOP_K):
        te = jnp.where(lane == k, idxs[k], te)
        tg = jnp.where(lane == k, es[k] / tot, tg)
    te_ref[...] = te.astype(jnp.int32)
    tg_ref[...] = tg


def _merge(oa, ob, oc, gm, x2, g1, n2g, sc2, sh2, wb, wo, rwh, rwl, rb, seq):
    N, D = x2.shape
    tm = 256
    per = seq // tm
    row = lambda w: pl.BlockSpec((tm, w), lambda i: (i, 0))
    mod = pl.BlockSpec((1, 1, D), lambda i: (i // per, 0, 0))
    full = lambda a: pl.BlockSpec(a.shape, lambda i: (0,) * a.ndim)
    return pl.pallas_call(
        _merge_kernel,
        grid=(N // tm,),
        in_specs=[row(oa.shape[1]), row(ob.shape[1]), row(oc.shape[1]), row(GM_WIDTH), row(D),
                  mod, full(n2g), mod, mod, full(wb), full(wo), full(rwh), full(rwl), full(rb)],
        out_specs=[row(D), row(D), row(LANES), row(LANES)],
        out_shape=[jax.ShapeDtypeStruct((N, D), F32), jax.ShapeDtypeStruct((N, D), F32),
                   jax.ShapeDtypeStruct((N, LANES), jnp.int32), jax.ShapeDtypeStruct((N, LANES), F32)],
        compiler_params=_params(("parallel",)),
        name="merge_router",
    )(oa, ob, oc, gm, x2, g1, n2g, sc2, sh2, wb, wo, rwh, rwl, rb)


DISPATCH_TOKENS = 256


def _dispatch_kernel(dest_ref, h_hbm, xs_in_hbm, xs_hbm, sem):
    del xs_in_hbm
    base = pl.program_id(0) * DISPATCH_TOKENS

    def row_copy(t, d):
        return pltpu.make_async_copy(h_hbm.at[pl.ds(t, 1)], xs_hbm.at[pl.ds(d, 1)], sem)

    def issue(tt, carry):
        t = base + tt
        for k in range(TOP_K):
            row_copy(t, dest_ref[t * TOP_K + k]).start()
        return carry

    def drain(tt, carry):
        for k in range(TOP_K):
            row_copy(0, 0).wait()
        return carry

    lax.fori_loop(0, DISPATCH_TOKENS, issue, 0)
    lax.fori_loop(0, DISPATCH_TOKENS, drain, 0)


def _dispatch(dest_flat, h, xs_zero):
    N = h.shape[0]
    return pl.pallas_call(
        _dispatch_kernel,
        grid_spec=pltpu.PrefetchScalarGridSpec(
            num_scalar_prefetch=1,
            grid=(N // DISPATCH_TOKENS,),
            in_specs=[pl.BlockSpec(memory_space=pl.ANY), pl.BlockSpec(memory_space=pl.ANY)],
            out_specs=pl.BlockSpec(memory_space=pl.ANY),
            scratch_shapes=[pltpu.SemaphoreType.DMA(())]),
        out_shape=jax.ShapeDtypeStruct(xs_zero.shape, xs_zero.dtype),
        input_output_aliases={2: 0},
        compiler_params=pltpu.CompilerParams(dimension_semantics=("arbitrary",), has_side_effects=True),
        name="moe_dispatch",
    )(dest_flat, h, xs_zero)


MOE_ROWS = 256


def _expert_kernel(be_ref, nu_ref, xs_ref, w1_ref, b1_ref, w2_ref, b2_ref, ys_ref):
    b = pl.program_id(0)
    F = w2_ref.shape[1]

    @pl.when(b < nu_ref[0])
    def _():
        hb = _dot(xs_ref[...].astype(BF16), w1_ref[0]) + b1_ref[0]
        g = jnp.minimum(hb[:, :F], SWIGLU_LIMIT)
        u = jnp.clip(hb[:, F:], -SWIGLU_LIMIT, SWIGLU_LIMIT)
        act = (u + 1.0) * (g * jax.nn.sigmoid(g * SWIGLU_ALPHA))
        ys_ref[...] = _dot(act.astype(BF16), w2_ref[0]) + b2_ref[0]

    @pl.when(b >= nu_ref[0])
    def _():
        ys_ref[...] = jnp.zeros_like(ys_ref)


def _experts(block_expert, n_used, xs, w1, b1, w2, b2):
    P, D = xs.shape
    E, _, F2 = w1.shape
    F = F2 // 2
    blk = lambda b, be, nu: jnp.minimum(b, nu[0] - 1)
    exp = lambda b, be, nu: be[jnp.minimum(b, nu[0] - 1)]
    return pl.pallas_call(
        _expert_kernel,
        grid_spec=pltpu.PrefetchScalarGridSpec(
            num_scalar_prefetch=2,
            grid=(P // MOE_ROWS,),
            in_specs=[pl.BlockSpec((MOE_ROWS, D), lambda b, be, nu: (blk(b, be, nu), 0)),
                      pl.BlockSpec((1, D, F2), lambda b, be, nu: (exp(b, be, nu), 0, 0)),
                      pl.BlockSpec((1, 1, F2), lambda b, be, nu: (exp(b, be, nu), 0, 0)),
                      pl.BlockSpec((1, F, D), lambda b, be, nu: (exp(b, be, nu), 0, 0)),
                      pl.BlockSpec((1, 1, D), lambda b, be, nu: (exp(b, be, nu), 0, 0))],
            out_specs=pl.BlockSpec((MOE_ROWS, D), lambda b, be, nu: (b, 0))),
        out_shape=jax.ShapeDtypeStruct((P, D), F32),
        compiler_params=_params(("arbitrary",)),
        name="moe_experts",
    )(block_expert, n_used, xs, w1, b1.reshape(E, 1, F2), w2, b2.reshape(E, 1, D))


COMBINE_TOKENS = 128


def _combine_kernel(dest_ref, ys_hbm, x_ref, tg_ref, g2_ref, fg_ref, o_ref, buf, sem, *, final):
    s = pl.program_id(0)
    n = pl.num_programs(0)
    T = COMBINE_TOKENS

    def row_copy(d, slot, k, tt):
        return pltpu.make_async_copy(ys_hbm.at[pl.ds(d, 1)], buf.at[slot, k, pl.ds(tt, 1)], sem.at[slot])

    def issue(step, slot):
        def body(tt, carry):
            t = step * T + tt
            for k in range(TOP_K):
                row_copy(dest_ref[t * TOP_K + k], slot, k, tt).start()
            return carry
        lax.fori_loop(0, T, body, 0)

    @pl.when(s == 0)
    def _():
        issue(0, 0)

    @pl.when(s + 1 < n)
    def _():
        issue(s + 1, (s + 1) % 2)

    slot = s % 2

    def drain(tt, carry):
        for k in range(TOP_K):
            row_copy(0, slot, k, tt).wait()
        return carry
    lax.fori_loop(0, T, drain, 0)

    tg = tg_ref[...]
    y = tg[:, 0:1] * buf[slot, 0]
    for k in range(1, TOP_K):
        y = y + tg[:, k:k + 1] * buf[slot, k]
    xn = x_ref[...] + g2_ref[0] * y
    if final:
        var = jnp.mean(xn * xn, axis=-1, keepdims=True)
        xn = xn * lax.rsqrt(var + EPS) * fg_ref[...]
    o_ref[...] = xn


def _combine(dest_flat, ys, x2, tg, g2, final_g, seq, final):
    N, D = x2.shape
    T = COMBINE_TOKENS
    per = seq // T
    kern = functools.partial(_combine_kernel, final=final)
    return pl.pallas_call(
        kern,
        grid_spec=pltpu.PrefetchScalarGridSpec(
            num_scalar_prefetch=1,
            grid=(N // T,),
            in_specs=[pl.BlockSpec(memory_space=pl.ANY),
                      pl.BlockSpec((T, D), lambda i, d: (i, 0)),
                      pl.BlockSpec((T, LANES), lambda i, d: (i, 0)),
                      pl.BlockSpec((1, 1, D), lambda i, d: (i // per, 0, 0)),
                      pl.BlockSpec((1, D), lambda i, d: (0, 0))],
            out_specs=pl.BlockSpec((T, D), lambda i, d: (i, 0)),
            scratch_shapes=[pltpu.VMEM((2, TOP_K, T, D), F32), pltpu.SemaphoreType.DMA((2,))]),
        out_shape=jax.ShapeDtypeStruct((N, D), F32),
        compiler_params=_params(("arbitrary",)),
        name="moe_combine",
    )(dest_flat, ys, x2, tg, g2, final_g)


def _inproj_columns():
    hd = HEAD_DIM
    heads = lambda base: np.concatenate([base + h * hd + np.arange(hd) for h in HEAD_PERM])
    seg = lambda base, w: base + np.arange(w)
    cols = np.concatenate([
        heads(0), seg(768, 512), heads(2304),
        seg(1280, 512), seg(1792, 512),
        seg(512, 128), seg(640, 128),
        seg(2816, 128), seg(2944, 128), seg(3072, 128), seg(3200, 128), seg(3328, 128), seg(3456, 128)])
    scale = np.ones((ZQ_WIDTH,), np.float32)
    scale[:3 * 512] = HEAD_DIM ** -0.5
    return cols, scale


def _prep_inproj(w_in, b_in):
    cols, scale = _inproj_columns()
    D = w_in.shape[0]
    g0 = ZQ_WIDTH
    n_g = NSA_HEADS * 3
    w = jnp.concatenate([w_in[:, cols] * scale, w_in[:, g0:g0 + n_g], jnp.zeros((D, GN_WIDTH - n_g), F32),
                         w_in[:, g0 + n_g:]], axis=1).astype(BF16)
    b = jnp.concatenate([b_in[cols] * scale, b_in[g0:g0 + n_g], jnp.zeros((GN_WIDTH - n_g,), F32),
                         b_in[g0 + n_g:]]).reshape(1, -1)
    return w, b


def _overlap_matrix(S, NC):
    nc = (S - CMP_BLOCK) // CMP_STRIDE + 1
    nsb = S // SEL_BLOCK
    cstart = np.arange(nc) * CMP_STRIDE
    sstart = np.arange(nsb) * SEL_BLOCK
    ov = np.clip(np.minimum(cstart[:, None] + CMP_BLOCK, sstart[None, :] + SEL_BLOCK)
                 - np.maximum(cstart[:, None], sstart[None, :]), 0, None).astype(np.float32) / CMP_BLOCK
    out = np.zeros((NC, LANES), np.float32)
    out[:nc, :nsb] = ov
    return jnp.asarray(out, BF16)


def _routing(te, n_tokens):
    E = N_EXPERTS
    onehot = (te[:, :, None] == jnp.arange(E, dtype=jnp.int32)[None, None, :]).astype(jnp.int32).sum(axis=1)
    csum = jnp.cumsum(onehot, axis=0)
    counts = csum[-1]
    rank = jnp.take_along_axis(csum - onehot, te, axis=1)
    padded = (counts + MOE_ROWS - 1) // MOE_ROWS * MOE_ROWS
    pad_end = jnp.cumsum(padded)
    pad_start = pad_end - padded
    dest = (pad_start[te] + rank).astype(jnp.int32)
    n_blocks = n_tokens * TOP_K // MOE_ROWS + E
    block_expert = jnp.clip(jnp.searchsorted(pad_end, jnp.arange(n_blocks) * MOE_ROWS, side='right'),
                            0, E - 1).astype(jnp.int32)
    n_used = (pad_end[-1] // MOE_ROWS).astype(jnp.int32).reshape(1)
    return dest.reshape(-1), block_expert, n_used, n_blocks


def kernel(x, c, mod_w, mod_b, norm1_g, norm2_g, w_in, b_in, sinks, diff_lambda, diff_subln_g, cmp_pos,
           cmp_w1, cmp_b1, cmp_w2, cmp_b2, w_branch, w_out, router_w, router_b, exp_w1, exp_b1, exp_w2,
           exp_b2, final_g):
    B, S, D = x.shape
    L = mod_w.shape[0]
    N = B * S
    NC = S // CMP_STRIDE
    hd = HEAD_DIM
    perm_rows = np.concatenate([h * hd + np.arange(hd) for h in HEAD_PERM])
    overlap = _overlap_matrix(S, NC)

    mod = _modulation(c, mod_w, mod_b)
    x2 = x.reshape(N, D)
    for l in range(L):
        sh1, sc1, g1, sh2, sc2, g2 = [m.reshape(B, 1, D) for m in jnp.split(mod[l], 6, axis=-1)]
        lam_init = 0.8 - 0.6 * math.exp(-0.3 * l)

        w_l, b_l = _prep_inproj(w_in[l], b_in[l])
        zq, gn, gm = _inproj(x2, norm1_g[l].reshape(1, D), sc1, sh1, w_l, b_l, S)
        zq3 = zq.reshape(B, S, ZQ_WIDTH)

        o_a = _swa(zq3, sinks[l])
        o_b = _diff(zq3, diff_lambda[l], diff_subln_g[l], lam_init)

        def groups(cb):
            a = zq3[:, :, cb * LANES:(cb + 1) * LANES].reshape(B, S, 2, hd)
            return a.transpose(0, 2, 1, 3).reshape(B, 2, NC, CMP_STRIDE * hd)
        xg = jnp.stack([groups(CB_KCM), groups(CB_VCM)])
        zeros = jnp.zeros_like(cmp_w2[l])
        w2p = jnp.stack([jnp.concatenate([cmp_w2[l], zeros], axis=-1),
                         jnp.concatenate([zeros, cmp_w2[l]], axis=-1)], axis=1).astype(BF16)
        zb = jnp.zeros_like(cmp_b2[l])
        b2p = jnp.stack([jnp.concatenate([cmp_b2[l], zb], axis=-1),
                         jnp.concatenate([zb, cmp_b2[l]], axis=-1)], axis=1).reshape(2, 2, 1, LANES)
        cmp_kv = _compress(xg, cmp_pos[l].reshape(2, 1, CMP_BLOCK * hd), cmp_w1[l].astype(BF16),
                           cmp_b1[l].reshape(2, 1, CMP_HIDDEN), w2p, b2p)
        o_c = _nsa(zq3, cmp_kv, gn.reshape(B, S, GN_WIDTH), overlap)

        wb = jnp.stack([w_branch[l, 0][perm_rows], w_branch[l, 1], w_branch[l, 2][perm_rows]]).astype(BF16)
        rw = jnp.concatenate([router_w[l], jnp.zeros((D, LANES - N_EXPERTS), F32)], axis=1)
        rw_hi = rw.astype(BF16)
        rw_lo = (rw - rw_hi.astype(F32)).astype(BF16)
        rb = jnp.concatenate([router_b[l], jnp.full((LANES - N_EXPERTS,), NEG, F32)]).reshape(1, LANES)
        x2, h2, te, tg = _merge(o_a.reshape(N, -1), o_b.reshape(N, -1), o_c.reshape(N, -1), gm, x2, g1,
                                norm2_g[l].reshape(1, D), sc2, sh2, wb, w_out[l].astype(BF16),
                                rw_hi, rw_lo, rb, S)

        dest, block_expert, n_used, n_blocks = _routing(te[:, :TOP_K], N)
        xs = _dispatch(dest, h2, jnp.zeros((n_blocks * MOE_ROWS, D), F32))
        ys = _experts(block_expert, n_used, xs, exp_w1[l].astype(BF16), exp_b1[l],
                      exp_w2[l].astype(BF16), exp_b2[l])
        x2 = _combine(dest, ys, x2, tg, g2, final_g.reshape(1, D), S, final=(l == L - 1))
    return x2.reshape(B, S, D)
```
